```python
import jax, jax.numpy as jnp
from jax import lax
import numpy as np

D_MODEL = 1024
BATCH = 2
SEQ = 8192
DEPTH = 2

EXPAND = 2
D_INNER = EXPAND * D_MODEL
EPS = 1e-6
NEG = -1e30

A_WIDTH = D_INNER // 2
POOL_SIZES = (2, 4, 8, 16)
A_GROUPS = len(POOL_SIZES)
A_GROUP_DIM = A_WIDTH // A_GROUPS
B_WIDTH = D_INNER - A_WIDTH
B_GROUPS = 4
B_GROUP_DIM = B_WIDTH // B_GROUPS
CHUNK = 128
EVEN_IN = 2 * A_WIDTH + 3 * B_WIDTH

DILATED = ((128, 1), (512, 4), (2048, 16))
N_DIL = len(DILATED)
HEAD_DIM = 128
C_SLOTS = 8
C_HEADS = C_SLOTS * N_DIL
C_WIDTH = C_SLOTS * HEAD_DIM
D_WIDTH = D_INNER - C_WIDTH
CONV_W = 3
ATTN_BLOCK = 128
QKV_WIDTH = 3 * C_HEADS * HEAD_DIM
ODD_IN = QKV_WIDTH + C_WIDTH + 4 * D_WIDTH

N_EVEN = (DEPTH + 1) // 2
N_ODD = DEPTH // 2

kernel_name = "hybrid_pool_gmlp_dilattn_shortconv"


def rmsnorm(x, g):
    xf = x.astype(jnp.float32)
    y = xf * lax.rsqrt(jnp.mean(xf * xf, axis=-1, keepdims=True) + EPS)
    return (y * g.astype(jnp.float32)).astype(x.dtype)


def alibi_slopes(n):
    return jnp.asarray(2.0 ** (-8.0 * (np.arange(n) + 1) / n), dtype=jnp.float32)


def multiscale_pool(a, pool_w, pool_scale):
    Bn, S, _ = a.shape
    ag = a.reshape(Bn, S, A_GROUPS, A_GROUP_DIM).astype(jnp.float32)
    cs = jnp.pad(jnp.cumsum(ag, axis=1), ((0, 0), (1, 0), (0, 0), (0, 0)))
    t = jnp.arange(S)
    means = []
    for g, w in enumerate(POOL_SIZES):
        lo = jnp.maximum(t + 1 - w, 0)
        cnt = (t + 1 - lo).astype(jnp.float32)
        means.append((cs[:, 1:, g] - cs[:, lo, g]) / cnt[None, :, None])
    pooled = (jnp.stack(means, axis=2) - ag).astype(a.dtype)
    mixed = jnp.einsum('bsgc,gcd->bsgd', pooled, pool_w)
    return mixed.reshape(Bn, S, A_WIDTH) * pool_scale


def chunk_spatial_gate(u, v, ws, bs):
    Bn, S, _ = v.shape
    nc = S // CHUNK
    vg = v.reshape(Bn, nc, CHUNK, B_GROUPS, B_GROUP_DIM)
    causal = jnp.tril(jnp.ones((CHUNK, CHUNK), dtype=bool))
    w = jnp.where(causal[None], ws, jnp.zeros_like(ws))
    mixed = jnp.einsum('gts,bnsgc->bntgc', w, vg) + bs.T[None, None, :, :, None]
    return u * mixed.reshape(Bn, S, B_WIDTH)


def dilated_group_attention(q, k, v, window, dilation, slopes):
    Bn, S, H, Dh = q.shape
    unit = dilation * ATTN_BLOCK
    Sp = -(-S // unit) * unit
    pad = Sp - S
    L = Sp // dilation
    nb = L // ATTN_BLOCK
    span = window // dilation

    def to_sub(x):
        x = jnp.pad(x, ((0, 0), (0, pad), (0, 0), (0, 0)))
        return x.reshape(Bn, L, dilation, H, Dh).transpose(0, 2, 1, 3, 4)

    def band(x):
        x = jnp.pad(x, ((0, 0), (0, 0), (ATTN_BLOCK, 0), (0, 0), (0, 0)))
        x = x.reshape(Bn, dilation, nb + 1, ATTN_BLOCK, H, Dh)
        return jnp.concatenate([x[:, :, :-1], x[:, :, 1:]], axis=3)

    qb = to_sub(q).reshape(Bn, dilation, nb, ATTN_BLOCK, H, Dh)
    kb = band(to_sub(k))
    vb = band(to_sub(v))
    s = jnp.einsum('bdnqhe,bdnkhe->bdnhqk', qb, kb,
                   preferred_element_type=jnp.float32) * (Dh ** -0.5)
    qi = jnp.arange(ATTN_BLOCK)[:, None] + ATTN_BLOCK
    ki = jnp.arange(2 * ATTN_BLOCK)[None, :]
    steps = qi - ki
    band_ok = (steps >= 0) & (steps <= span)
    blk = jnp.arange(nb)[:, None, None]
    valid = band_ok[None] & ((blk > 0) | (ki >= ATTN_BLOCK)[None])
    dist = (steps * dilation).astype(jnp.float32)
    s = s - slopes[:, None, None] * dist
    s = jnp.where(valid[:, None], s, NEG)
    lse = jax.nn.logsumexp(s, axis=-1)
    p = jnp.exp(s - lse[..., None])
    o = jnp.einsum('bdnhqk,bdnkhe->bdnqhe', p.astype(v.dtype), vb)
    o = o.reshape(Bn, dilation, L, H, Dh).transpose(0, 2, 1, 3, 4).reshape(Bn, Sp, H, Dh)[:, :S]
    lse = lse.transpose(0, 1, 2, 4, 3).reshape(Bn, dilation, L, H)
    lse = lse.transpose(0, 2, 1, 3).reshape(Bn, Sp, H)[:, :S]
    return o, lse


def short_gated_conv(gb, gc, xt, conv_w):
    S = xt.shape[1]
    z = jnp.pad(gc * xt, ((0, 0), (CONV_W - 1, 0), (0, 0)))
    conv = conv_w[0] * z[:, 0:S]
    for j in range(1, CONV_W):
        conv = conv + conv_w[j] * z[:, j:j + S]
    return gb * conv


def even_layer(x, norm_g, w_in, pool_w, pool_scale, ws, bs, w_out):
    h = rmsnorm(x, norm_g)
    z = h @ w_in
    a, g_a, u, v, g_b = jnp.split(
        z, [A_WIDTH, 2 * A_WIDTH, 2 * A_WIDTH + B_WIDTH, 2 * A_WIDTH + 2 * B_WIDTH], axis=-1)
    y_a = multiscale_pool(a, pool_w, pool_scale) * jax.nn.silu(g_a)
    y_b = chunk_spatial_gate(u, v, ws, bs) * jax.nn.silu(g_b)
    return x + jnp.concatenate([y_a, y_b], axis=-1) @ w_out


def odd_layer(x, norm_g, w_in, conv_w, w_out):
    Bn, S, _ = x.shape
    h = rmsnorm(x, norm_g)
    z = h @ w_in
    o1 = QKV_WIDTH
    o2 = o1 + C_WIDTH
    qkv, g_c, d_b, d_c, d_x, g_d = jnp.split(
        z, [o1, o2, o2 + D_WIDTH, o2 + 2 * D_WIDTH, o2 + 3 * D_WIDTH], axis=-1)
    qkv = qkv.reshape(Bn, S, 3, N_DIL, C_SLOTS, HEAD_DIM)
    slopes = alibi_slopes(C_HEADS).reshape(N_DIL, C_SLOTS)
    outs, lses = [], []
    for gi, (window, dil) in enumerate(DILATED):
        o, l = dilated_group_attention(qkv[:, :, 0, gi], qkv[:, :, 1, gi], qkv[:, :, 2, gi],
                                       window, dil, slopes[gi])
        outs.append(o)
        lses.append(l)
    alpha = jax.nn.softmax(jnp.stack(lses, axis=0), axis=0)
    y_c = jnp.einsum('gbsh,gbshe->bshe', alpha.astype(x.dtype), jnp.stack(outs, axis=0))
    y_c = y_c.reshape(Bn, S, C_WIDTH) * jax.nn.silu(g_c)
    y_d = short_gated_conv(d_b, d_c, d_x, conv_w) * jax.nn.silu(g_d)
    return x + jnp.concatenate([y_c, y_d], axis=-1) @ w_out


def setup_inputs(seed: int = 0) -> dict:
    key = jax.random.key(seed)
    ks = jax.random.split(key, 16)
    nrm = jax.random.normal
    f32 = jnp.float32
    return {
        "x": nrm(ks[0], (BATCH, SEQ, D_MODEL), f32),
        "even_norm": 1.0 + 0.05 * nrm(ks[1], (N_EVEN, D_MODEL), f32),
        "even_w_in": nrm(ks[2], (N_EVEN, D_MODEL, EVEN_IN), f32) * D_MODEL ** -0.5,
        "even_pool_w": nrm(ks[3], (N_EVEN, A_GROUPS, A_GROUP_DIM, A_GROUP_DIM), f32) * A_GROUP_DIM ** -0.5,
        "even_pool_scale": 1.0 + 0.1 * nrm(ks[4], (N_EVEN, A_WIDTH), f32),
        "even_ws": nrm(ks[5], (N_EVEN, B_GROUPS, CHUNK, CHUNK), f32) * CHUNK ** -0.5,
        "even_bs": 1.0 + 0.1 * nrm(ks[6], (N_EVEN, B_GROUPS, CHUNK), f32),
        "even_w_out": nrm(ks[7], (N_EVEN, D_INNER, D_MODEL), f32) * D_INNER ** -0.5,
        "odd_norm": 1.0 + 0.05 * nrm(ks[8], (N_ODD, D_MODEL), f32),
        "odd_w_in": nrm(ks[9], (N_ODD, D_MODEL, ODD_IN), f32) * D_MODEL ** -0.5,
        "odd_conv_w": nrm(ks[10], (N_ODD, CONV_W, D_WIDTH), f32) * CONV_W ** -0.5,
        "odd_w_out": nrm(ks[11], (N_ODD, D_INNER, D_MODEL), f32) * D_INNER ** -0.5,
        "final_norm": 1.0 + 0.05 * nrm(ks[12], (D_MODEL,), f32),
    }


def reference(x, even_norm, even_w_in, even_pool_w, even_pool_scale, even_ws, even_bs,
              even_w_out, odd_norm, odd_w_in, odd_conv_w, odd_w_out, final_norm):
    h = x
    for layer in range(DEPTH):
        i = layer // 2
        if layer % 2 == 0:
            h = even_layer(h, even_norm[i], even_w_in[i], even_pool_w[i], even_pool_scale[i],
                           even_ws[i], even_bs[i], even_w_out[i])
        else:
            h = odd_layer(h, odd_norm[i], odd_w_in[i], odd_conv_w[i], odd_w_out[i])
    return rmsnorm(h, final_norm)
```

```python
import functools

import jax
import jax.numpy as jnp
import numpy as np
from jax import lax
from jax.experimental import pallas as pl
from jax.experimental.pallas import tpu as pltpu

D_MODEL = 1024
D_INNER = 2048
EPS = 1e-6
NEG = -1e30

A_WIDTH = 1024
POOL_SIZES = (2, 4, 8, 16)
GROUP_DIM = 256
N_GROUPS = 4
B_WIDTH = 1024
CHUNK = 128
POOL_HALO = 16

DILATED = ((128, 1), (512, 4), (2048, 16))
N_DIL = 3
HEAD_DIM = 128
C_SLOTS = 8
C_HEADS = C_SLOTS * N_DIL
C_WIDTH = 1024
D_WIDTH = 1024
CONV_W = 3
CONV_HALO = 8
ATTN_BLOCK = 128
QKV_WIDTH = 3 * C_HEADS * HEAD_DIM

ROW_TILE = 512
VMEM_LIMIT_BYTES = 56 * 1024 * 1024

BF16 = jnp.bfloat16
F32 = jnp.float32


def _params(n_axes):
    return pltpu.CompilerParams(
        dimension_semantics=("arbitrary",) * n_axes,
        vmem_limit_bytes=VMEM_LIMIT_BYTES)


def _rmsnorm(x, g):
    y = x * lax.rsqrt(jnp.mean(x * x, axis=-1, keepdims=True) + EPS)
    return y * g


def _silu(x):
    return x * (1.0 / (1.0 + jnp.exp(-x)))


def _dot(a, b):
    return jnp.dot(a, b, preferred_element_type=F32)


def _shift_rows(x, k):
    return pltpu.roll(x, k, 0)


def _even_kernel(x_ref, g_ref, win_ref, poolw_ref, pscale_ref, ws_ref, bst_ref, wout_ref,
                 o_ref, halo_ref, y_ref):
    tm = x_ref.shape[0]
    j = pl.program_id(1)

    @pl.when(j == 0)
    def _():
        halo_ref[...] = jnp.zeros_like(halo_ref)

    x = x_ref[...]
    h = _rmsnorm(x, g_ref[...]).astype(BF16)
    t_abs = j * tm + lax.broadcasted_iota(jnp.int32, (tm, 1), 0)

    for g, w in enumerate(POOL_SIZES):
        cols = slice(g * GROUP_DIM, (g + 1) * GROUP_DIM)
        a = _dot(h, win_ref[:, cols])
        gate = _dot(h, win_ref[:, A_WIDTH + g * GROUP_DIM:A_WIDTH + (g + 1) * GROUP_DIM])
        s = jnp.concatenate([halo_ref[:, cols], a], axis=0)
        k = 1
        while k < w:
            s = s + _shift_rows(s, k)
            k *= 2
        inv_cnt = 1.0 / jnp.minimum(t_abs + 1, w).astype(F32)
        pooled = s[POOL_HALO:] * inv_cnt - a
        mixed = _dot(pooled.astype(BF16), poolw_ref[g])
        y_ref[:, cols] = (mixed * pscale_ref[:, cols] * _silu(gate)).astype(BF16)
        halo_ref[:, cols] = a[tm - POOL_HALO:]

    row = lax.broadcasted_iota(jnp.int32, (CHUNK, CHUNK), 0)
    col = lax.broadcasted_iota(jnp.int32, (CHUNK, CHUNK), 1)
    off = 2 * A_WIDTH
    for g in range(N_GROUPS):
        u = _dot(h, win_ref[:, off + g * GROUP_DIM:off + (g + 1) * GROUP_DIM])
        v = _dot(h, win_ref[:, off + B_WIDTH + g * GROUP_DIM:off + B_WIDTH + (g + 1) * GROUP_DIM])
        gate = _dot(h, win_ref[:, off + 2 * B_WIDTH + g * GROUP_DIM:
                               off + 2 * B_WIDTH + (g + 1) * GROUP_DIM])
        w_tril = jnp.where(row >= col, ws_ref[g], 0.0).astype(BF16)
        bias = bst_ref[:, g:g + 1]
        vb = v.astype(BF16)
        mixed = jnp.concatenate(
            [_dot(w_tril, vb[c * CHUNK:(c + 1) * CHUNK]) + bias for c in range(tm // CHUNK)],
            axis=0)
        y_ref[:, A_WIDTH + g * GROUP_DIM:A_WIDTH + (g + 1) * GROUP_DIM] = (
            u * mixed * _silu(gate)).astype(BF16)

    o_ref[...] = x + _dot(y_ref[...], wout_ref[...])


def _even_layer(x, norm_g, w_in, pool_w, pool_scale, ws, bs, w_out):
    bn, s, d = x.shape
    tm = ROW_TILE
    const = lambda *shape: pl.BlockSpec(shape, lambda b, j: (0,) * len(shape))
    return pl.pallas_call(
        _even_kernel,
        grid=(bn, s // tm),
        in_specs=[
            pl.BlockSpec((None, tm, d), lambda b, j: (b, j, 0)),
            const(1, d),
            const(d, w_in.shape[1]),
            const(N_GROUPS, GROUP_DIM, GROUP_DIM),
            const(1, A_WIDTH),
            const(N_GROUPS, CHUNK, CHUNK),
            const(CHUNK, N_GROUPS),
            const(D_INNER, d),
        ],
        out_specs=pl.BlockSpec((None, tm, d), lambda b, j: (b, j, 0)),
        out_shape=jax.ShapeDtypeStruct(x.shape, F32),
        scratch_shapes=[pltpu.VMEM((POOL_HALO, A_WIDTH), F32),
                        pltpu.VMEM((tm, D_INNER), BF16)],
        compiler_params=_params(2),
        name="even_layer",
    )(x, norm_g.reshape(1, d), w_in.astype(BF16), pool_w.astype(BF16),
      pool_scale.reshape(1, A_WIDTH), ws, bs.T, w_out.astype(BF16))


def _qkv_kernel(x_ref, g_ref, w_ref, q_ref, k_ref, v_ref):
    h = _rmsnorm(x_ref[...], g_ref[...]).astype(BF16)
    q_ref[...] = (_dot(h, w_ref[:, 0:C_WIDTH]) * (HEAD_DIM ** -0.5)).astype(BF16)
    k_ref[...] = _dot(h, w_ref[:, C_WIDTH:2 * C_WIDTH]).astype(BF16)
    v_ref[...] = _dot(h, w_ref[:, 2 * C_WIDTH:3 * C_WIDTH]).astype(BF16)


def _qkv_project(x, norm_g, w_qkv, dil):
    bn, s, d = x.shape
    sub = s // dil
    tl = min(ROW_TILE, sub)
    x_sub = x.reshape(bn, sub, dil * d)
    out = jax.ShapeDtypeStruct((bn, dil, sub, C_WIDTH), BF16)
    out_spec = pl.BlockSpec((None, None, tl, C_WIDTH), lambda b, r, i: (b, r, i, 0))
    return pl.pallas_call(
        _qkv_kernel,
        grid=(bn, dil, sub // tl),
        in_specs=[
            pl.BlockSpec((None, tl, d), lambda b, r, i: (b, i, r)),
            pl.BlockSpec((1, d), lambda b, r, i: (0, 0)),
            pl.BlockSpec((d, 3 * C_WIDTH), lambda b, r, i: (0, 0)),
        ],
        out_specs=[out_spec, out_spec, out_spec],
        out_shape=[out, out, out],
        compiler_params=_params(3),
        name=f"qkv_project_d{dil}",
    )(x_sub, norm_g.reshape(1, d), w_qkv)


def _attn_kernel(q_ref, kp_ref, kc_ref, vp_ref, vc_ref, bias_ref, o_ref, lse_ref):
    i = pl.program_id(2)
    table = jnp.minimum(i, 1)
    lane = lax.broadcasted_iota(jnp.int32, (ATTN_BLOCK, 128), 1)
    lse_tile = jnp.zeros((ATTN_BLOCK, 128), F32)
    for hd in range(C_SLOTS):
        cols = slice(hd * HEAD_DIM, (hd + 1) * HEAD_DIM)
        keys = jnp.concatenate([kp_ref[:, cols], kc_ref[:, cols]], axis=0)
        vals = jnp.concatenate([vp_ref[:, cols], vc_ref[:, cols]], axis=0)
        s = lax.dot_general(q_ref[:, cols], keys, (((1,), (1,)), ((), ())),
                            preferred_element_type=F32)
        s = s + bias_ref[table, hd]
        m = jnp.max(s, axis=-1, keepdims=True)
        p = jnp.exp(s - m)
        l = jnp.sum(p, axis=-1, keepdims=True)
        o = _dot(p.astype(BF16), vals)
        o_ref[:, cols] = (o * (1.0 / l)).astype(BF16)
        lse_tile = jnp.where(lane == hd, m + jnp.log(l), lse_tile)
    lse_ref[...] = lse_tile


def _alibi_bias(group, dil, span):
    slopes = 2.0 ** (-8.0 * (np.arange(C_HEADS) + 1) / C_HEADS)
    slopes = slopes.astype(np.float32).reshape(N_DIL, C_SLOTS)[group]
    qi = np.arange(ATTN_BLOCK)[:, None] + ATTN_BLOCK
    ki = np.arange(2 * ATTN_BLOCK)[None, :]
    steps = qi - ki
    band = (steps >= 0) & (steps <= span)
    dist = (steps * dil).astype(np.float32)
    tables = []
    for first in (True, False):
        valid = band & ((ki >= ATTN_BLOCK) | (not first))
        tables.append(np.where(valid[None], -slopes[:, None, None] * dist[None],
                               np.float32(NEG)).astype(np.float32))
    return jnp.asarray(np.stack(tables, axis=0))


def _group_attention(q, k, v, group, window, dil):
    bn, _, sub, _ = q.shape
    s = sub * dil
    nb = sub // ATTN_BLOCK
    blk = lambda f: pl.BlockSpec((None, None, ATTN_BLOCK, C_WIDTH), f)
    cur = lambda b, r, i: (b, r, i, 0)
    prev = lambda b, r, i: (b, r, jnp.maximum(i - 1, 0), 0)
    bias = _alibi_bias(group, dil, window // dil)
    o, lse = pl.pallas_call(
        _attn_kernel,
        grid=(bn, dil, nb),
        in_specs=[blk(cur), blk(prev), blk(cur), blk(prev), blk(cur),
                  pl.BlockSpec(bias.shape, lambda b, r, i: (0, 0, 0, 0))],
        out_specs=[pl.BlockSpec((None, ATTN_BLOCK, C_WIDTH), lambda b, r, i: (b, i, r)),
                   pl.BlockSpec((None, ATTN_BLOCK, 128), lambda b, r, i: (b, i, r))],
        out_shape=[jax.ShapeDtypeStruct((bn, sub, dil * C_WIDTH), BF16),
                   jax.ShapeDtypeStruct((bn, sub, dil * 128), F32)],
        compiler_params=_params(3),
        name=f"dilated_attention_d{dil}",
    )(q, k, k, v, v, bias)
    return o.reshape(bn, s, C_WIDTH), lse.reshape(bn, s, 128)


def _odd_out_kernel(x_ref, g_ref, win_ref, convw_ref, wout_ref, fg_ref,
                    o0_ref, o1_ref, o2_ref, l0_ref, l1_ref, l2_ref,
                    out_ref, halo_ref, y_ref, *, final):
    tm = x_ref.shape[0]
    j = pl.program_id(1)

    @pl.when(j == 0)
    def _():
        halo_ref[...] = jnp.zeros_like(halo_ref)

    x = x_ref[...]
    h = _rmsnorm(x, g_ref[...]).astype(BF16)

    l0, l1, l2 = l0_ref[...], l1_ref[...], l2_ref[...]
    mx = jnp.maximum(jnp.maximum(l0, l1), l2)
    e0, e1, e2 = jnp.exp(l0 - mx), jnp.exp(l1 - mx), jnp.exp(l2 - mx)
    inv = 1.0 / (e0 + e1 + e2)
    a0, a1, a2 = e0 * inv, e1 * inv, e2 * inv
    for hd in range(C_SLOTS):
        cols = slice(hd * HEAD_DIM, (hd + 1) * HEAD_DIM)
        gate = _dot(h, win_ref[:, cols])
        merged = (a0[:, hd:hd + 1] * o0_ref[:, cols].astype(F32)
                  + a1[:, hd:hd + 1] * o1_ref[:, cols].astype(F32)
                  + a2[:, hd:hd + 1] * o2_ref[:, cols].astype(F32))
        y_ref[:, cols] = (merged * _silu(gate)).astype(BF16)

    for g in range(D_WIDTH // GROUP_DIM):
        cols = slice(g * GROUP_DIM, (g + 1) * GROUP_DIM)
        wcol = lambda part: win_ref[:, C_WIDTH + part * D_WIDTH + g * GROUP_DIM:
                                    C_WIDTH + part * D_WIDTH + (g + 1) * GROUP_DIM]
        d_b = _dot(h, wcol(0))
        z = _dot(h, wcol(1)) * _dot(h, wcol(2))
        gate = _dot(h, wcol(3))
        ext = jnp.concatenate([halo_ref[:, cols], z], axis=0)
        z1 = _shift_rows(ext, 1)[CONV_HALO:]
        z2 = _shift_rows(ext, 2)[CONV_HALO:]
        conv = (convw_ref[0:1, cols] * z2 + convw_ref[1:2, cols] * z1
                + convw_ref[2:3, cols] * z)
        y_ref[:, C_WIDTH + g * GROUP_DIM:C_WIDTH + (g + 1) * GROUP_DIM] = (
            d_b * conv * _silu(gate)).astype(BF16)
        halo_ref[:, cols] = z[tm - CONV_HALO:]

    out = x + _dot(y_ref[...], wout_ref[...])
    if final:
        out = _rmsnorm(out, fg_ref[...])
    out_ref[...] = out


def _odd_layer(x, norm_g, w_in, conv_w, w_out, final_g, final):
    bn, s, d = x.shape
    norm_g2 = norm_g.reshape(1, d)
    w_bf = w_in.astype(BF16)
    w_qkv = w_bf[:, :QKV_WIDTH].reshape(d, 3, N_DIL, C_WIDTH)
    outs, lses = [], []
    for gi, (window, dil) in enumerate(DILATED):
        q, k, v = _qkv_project(x, norm_g2, w_qkv[:, :, gi].reshape(d, 3 * C_WIDTH), dil)
        o, lse = _group_attention(q, k, v, gi, window, dil)
        outs.append(o)
        lses.append(lse)

    tm = ROW_TILE
    const = lambda *shape: pl.BlockSpec(shape, lambda b, j: (0,) * len(shape))
    tile = lambda width: pl.BlockSpec((None, tm, width), lambda b, j: (b, j, 0))
    w_rest = w_bf[:, QKV_WIDTH:]
    return pl.pallas_call(
        functools.partial(_odd_out_kernel, final=final),
        grid=(bn, s // tm),
        in_specs=[tile(d), const(1, d), const(d, w_rest.shape[1]), const(CONV_W, D_WIDTH),
                  const(D_INNER, d), const(1, d),
                  tile(C_WIDTH), tile(C_WIDTH), tile(C_WIDTH), tile(128), tile(128), tile(128)],
        out_specs=tile(d),
        out_shape=jax.ShapeDtypeStruct(x.shape, F32),
        scratch_shapes=[pltpu.VMEM((CONV_HALO, D_WIDTH), F32),
                        pltpu.VMEM((tm, D_INNER), BF16)],
        compiler_params=_params(2),
        name="odd_merge_conv_out",
    )(x, norm_g2, w_rest, conv_w, w_out.astype(BF16), final_g.reshape(1, d), *outs, *lses)


def kernel(x, even_norm, even_w_in, even_pool_w, even_pool_scale, even_ws, even_bs, even_w_out,
           odd_norm, odd_w_in, odd_conv_w, odd_w_out, final_norm):
    assert even_norm.shape[0] == 1 and odd_norm.shape[0] == 1, "two-layer trunk only"
    h = _even_layer(x, even_norm[0], even_w_in[0], even_pool_w[0], even_pool_scale[0],
                    even_ws[0], even_bs[0], even_w_out[0])
    return _odd_layer(h, odd_norm[0], odd_w_in[0], odd_conv_w[0], odd_w_out[0], final_norm,
                      final=True)
```
